```python
import jax, jax.numpy as jnp
from jax import lax
import numpy as np

D_MODEL = 1024
BATCH = 16
SEQ = 2048
DEPTH = 1
DEC_BATCH = 8
DEC_SEQ = 64
PAST_LEN = 4096

CHUNK = 64
D_RNN = 1024
N_RNN_HEADS = 16
RNN_BLOCK = D_RNN // N_RNN_HEADS
CONV_W = 4
LRU_C = 8.0
N_HEADS = 16
HEAD_DIM = 64
D_ATTN = N_HEADS * HEAD_DIM
Q_BLOCK = 128
N_GROUPS = 4
EXPERTS_PER_GROUP = 8
N_EXPERTS = N_GROUPS * EXPERTS_PER_GROUP
TOP_K = 2
D_EXPERT = 512
MOE_BLOCK = 128
EPS = 1e-6
NEG_INF = -1e30
D_IN = 2 * D_RNN + 3 * D_ATTN + N_HEADS + 2 * D_MODEL
SPLIT_POINTS = (D_RNN, 2 * D_RNN, 2 * D_RNN + D_ATTN, 2 * D_RNN + 2 * D_ATTN, 2 * D_RNN + 3 * D_ATTN,
                2 * D_RNN + 3 * D_ATTN + N_HEADS, 2 * D_RNN + 3 * D_ATTN + N_HEADS + D_MODEL)

kernel_name = "hawk_fox_hier_moe_stream_step"


def _rmsnorm(x, g):
    xf = x.astype(jnp.float32)
    y = xf * lax.rsqrt(jnp.mean(xf * xf, axis=-1, keepdims=True) + EPS)
    return (y * g.astype(jnp.float32)).astype(x.dtype)


def _lru_combine(left, right):
    a1, b1 = left
    a2, b2 = right
    return a1 * a2, a2 * b1 + b2


def _rg_lru(u, h0, w_a, b_a, w_x, b_x, lru_lambda):
    B, T, _ = u.shape
    ub = u.reshape(B, T, N_RNN_HEADS, RNN_BLOCK)
    r = jax.nn.sigmoid((jnp.einsum('bthi,hij->bthj', ub, w_a).reshape(B, T, D_RNN) + b_a).astype(jnp.float32))
    i = jax.nn.sigmoid((jnp.einsum('bthi,hij->bthj', ub, w_x).reshape(B, T, D_RNN) + b_x).astype(jnp.float32))
    log_a = -LRU_C * r * jax.nn.softplus(-lru_lambda.astype(jnp.float32))
    a = jnp.exp(log_a)
    b = jnp.sqrt(-jnp.expm1(2.0 * log_a)) * (i * u.astype(jnp.float32))
    b = b.at[:, 0].add(a[:, 0] * h0.astype(jnp.float32))
    _, h = lax.associative_scan(_lru_combine, (a, b), axis=1)
    return h, h[:, -1]


def _fox_attention(q, k, v, c_q, c_k, past):
    B, T, H, DH = q.shape
    qb = min(Q_BLOCK, T)
    nb = T // qb
    k_pos = jnp.arange(k.shape[1])
    ckT = c_k.transpose(0, 2, 1)[:, :, None, :]
    scale = HEAD_DIM ** -0.5

    def block(args):
        q_blk, cq_blk, q_pos = args
        s = jnp.einsum('bqhd,bkhd->bhqk', q_blk, k).astype(jnp.float32) * scale
        s = s + cq_blk.astype(jnp.float32).transpose(0, 2, 1)[..., None] - ckT
        s = jnp.where(k_pos[None, :] <= q_pos[:, None], s, NEG_INF)
        p = jax.nn.softmax(s, axis=-1)
        return jnp.einsum('bhqk,bkhd->bqhd', p.astype(v.dtype), v)

    q_r = q.reshape(B, nb, qb, H, DH).transpose(1, 0, 2, 3, 4)
    cq_r = c_q.reshape(B, nb, qb, H).transpose(1, 0, 2, 3)
    pos_r = (past + jnp.arange(T)).reshape(nb, qb)
    out = lax.map(block, (q_r, cq_r, pos_r))
    return out.transpose(1, 0, 2, 3, 4).reshape(B, T, H, DH)


def _grouped_experts(xf, eid, gate, w_gate_e, w_up_e, w_down_e):
    N, D = xf.shape
    A = N * TOP_K
    flat_e = eid.reshape(-1)
    order = jnp.argsort(flat_e)
    se = flat_e[order]
    tok = (order // TOP_K).astype(jnp.int32)
    counts = jnp.bincount(flat_e, length=N_EXPERTS)
    padded = (counts + MOE_BLOCK - 1) // MOE_BLOCK * MOE_BLOCK
    start = jnp.cumsum(counts) - counts
    pend = jnp.cumsum(padded)
    pstart = pend - padded
    dest = pstart[se] + jnp.arange(A) - start[se]
    nblk = (A + MOE_BLOCK - 1) // MOE_BLOCK + N_EXPERTS
    R = nblk * MOE_BLOCK
    row_tok = jnp.full((R,), N, jnp.int32).at[dest].set(tok)
    row_gate = jnp.zeros((R,), xf.dtype).at[dest].set(gate.reshape(-1)[order].astype(xf.dtype))
    blk_e = jnp.minimum(jnp.searchsorted(pend, jnp.arange(nblk) * MOE_BLOCK, side='right'), N_EXPERTS - 1)
    xpad = jnp.concatenate([xf, jnp.zeros((1, D), xf.dtype)], axis=0)
    xb = xpad[row_tok].reshape(nblk, MOE_BLOCK, D)

    def expert_block(args):
        xblk, e = args
        hdn = jax.nn.silu(xblk @ w_gate_e[e]) * (xblk @ w_up_e[e])
        return hdn @ w_down_e[e]

    yb = lax.map(expert_block, (xb, blk_e)).reshape(R, D)
    y = jnp.zeros((N + 1, D), xf.dtype).at[row_tok].add(yb * row_gate[:, None])
    return y[:N]


def _hier_moe(h, w_route_group, b_route_group, w_route_expert, b_route_expert, w_gate_e, w_up_e, w_down_e):
    B, T, D = h.shape
    xf = h.reshape(-1, D)
    N = xf.shape[0]
    g_logits = (xf @ w_route_group).astype(jnp.float32) + b_route_group.astype(jnp.float32)
    g_prob = jax.nn.softmax(g_logits, axis=-1)
    grp = jnp.argmax(g_logits, axis=-1)
    g_w = jnp.take_along_axis(g_prob, grp[:, None], axis=1)
    e_logits = ((xf @ w_route_expert).astype(jnp.float32) + b_route_expert.astype(jnp.float32)).reshape(N, N_GROUPS, EXPERTS_PER_GROUP)
    e_in = jnp.take_along_axis(e_logits, grp[:, None, None], axis=1)[:, 0]
    top_v, top_i = lax.top_k(e_in, TOP_K)
    gate = jax.nn.softmax(top_v, axis=-1) * g_w
    eid = grp[:, None] * EXPERTS_PER_GROUP + top_i
    return _grouped_experts(xf, eid, gate, w_gate_e, w_up_e, w_down_e).reshape(B, T, D)


def _layer(x, conv_hist, h0, k_hist, v_hist, lf_hist,
           norm1_g, w_in, conv_w, conv_b, w_a, b_a, w_x, b_x, lru_lambda, b_f, q_norm_g, k_norm_g,
           w_rnn_out, w_attn_out, w_o, norm2_g, w_route_group, b_route_group, w_route_expert, b_route_expert,
           w_gate_e, w_up_e, w_down_e):
    B, T, _ = x.shape
    P = k_hist.shape[1]
    xn = _rmsnorm(x, norm1_g)
    z = xn @ w_in
    u_x, u_g, q, k, v, f_logit, g_rnn, g_attn = jnp.split(z, SPLIT_POINTS, axis=-1)
    conv_in = jnp.concatenate([conv_hist.astype(u_x.dtype), u_x], axis=1)
    conv_out = lax.conv_general_dilated(conv_in, conv_w.astype(u_x.dtype)[:, None, :], (1,), 'VALID',
                                        dimension_numbers=('NWC', 'WIO', 'NWC'),
                                        feature_group_count=D_RNN) + conv_b
    new_conv = conv_in[:, -(CONV_W - 1):]
    h_seq, h_last = _rg_lru(conv_out, h0, w_a, b_a, w_x, b_x, lru_lambda)
    rnn_y = (jax.nn.gelu(u_g) * h_seq.astype(x.dtype)) @ w_rnn_out
    q = _rmsnorm(q.reshape(B, T, N_HEADS, HEAD_DIM), q_norm_g)
    k = _rmsnorm(k.reshape(B, T, N_HEADS, HEAD_DIM), k_norm_g)
    v = v.reshape(B, T, N_HEADS, HEAD_DIM)
    lf = jax.nn.log_sigmoid(f_logit.astype(jnp.float32) + b_f.astype(jnp.float32))
    k_all = jnp.concatenate([k_hist.astype(k.dtype), k], axis=1)
    v_all = jnp.concatenate([v_hist.astype(v.dtype), v], axis=1)
    c_all = jnp.cumsum(jnp.concatenate([lf_hist.astype(jnp.float32), lf], axis=1), axis=1)
    attn = _fox_attention(q, k_all, v_all, c_all[:, P:], c_all, P)
    attn_y = attn.reshape(B, T, D_ATTN) @ w_attn_out
    x = x + (jax.nn.sigmoid(g_rnn) * rnn_y + jax.nn.sigmoid(g_attn) * attn_y) @ w_o
    x = x + _hier_moe(_rmsnorm(x, norm2_g), w_route_group, b_route_group, w_route_expert, b_route_expert,
                      w_gate_e, w_up_e, w_down_e)
    return x, new_conv, h_last, k, v, lf


def setup_inputs(seed: int = 0) -> dict:
    key = jax.random.key(seed)
    ks = jax.random.split(key, 32)
    n = jax.random.normal
    L = DEPTH
    u = jax.random.uniform(ks[14], (L, D_RNN), minval=0.9, maxval=0.999)
    a0 = u ** (1.0 / LRU_C)
    return {
        "x_prompt": n(ks[0], (BATCH, SEQ, D_MODEL), jnp.float32),
        "x_sample": n(ks[1], (DEC_BATCH, DEC_SEQ, D_MODEL), jnp.float32),
        "cache_k": n(ks[2], (L, DEC_BATCH, PAST_LEN, N_HEADS, HEAD_DIM), jnp.float32),
        "cache_v": n(ks[3], (L, DEC_BATCH, PAST_LEN, N_HEADS, HEAD_DIM), jnp.float32),
        "cache_logf": jax.nn.log_sigmoid(3.0 + n(ks[4], (L, DEC_BATCH, PAST_LEN, N_HEADS), jnp.float32)),
        "state_conv": n(ks[5], (L, DEC_BATCH, CONV_W - 1, D_RNN), jnp.float32),
        "state_rglru": 0.5 * n(ks[6], (L, DEC_BATCH, D_RNN), jnp.float32),
        "norm1_g": 1.0 + 0.05 * n(ks[7], (L, D_MODEL), jnp.float32),
        "w_in": n(ks[8], (L, D_MODEL, D_IN), jnp.float32) * D_MODEL ** -0.5,
        "conv_w": n(ks[9], (L, CONV_W, D_RNN), jnp.float32) * CONV_W ** -0.5,
        "conv_b": 0.01 * n(ks[10], (L, D_RNN), jnp.float32),
        "w_a": n(ks[11], (L, N_RNN_HEADS, RNN_BLOCK, RNN_BLOCK), jnp.float32) * RNN_BLOCK ** -0.5,
        "b_a": 0.01 * n(ks[12], (L, D_RNN), jnp.float32),
        "w_x": n(ks[13], (L, N_RNN_HEADS, RNN_BLOCK, RNN_BLOCK), jnp.float32) * RNN_BLOCK ** -0.5,
        "b_x": 0.01 * n(ks[15], (L, D_RNN), jnp.float32),
        "lru_lambda": jnp.log(a0) - jnp.log1p(-a0),
        "b_f": 3.0 + 0.1 * n(ks[16], (L, N_HEADS), jnp.float32),
        "q_norm_g": 1.0 + 0.05 * n(ks[17], (L, HEAD_DIM), jnp.float32),
        "k_norm_g": 1.0 + 0.05 * n(ks[18], (L, HEAD_DIM), jnp.float32),
        "w_rnn_out": n(ks[19], (L, D_RNN, D_MODEL), jnp.float32) * D_RNN ** -0.5,
        "w_attn_out": n(ks[20], (L, D_ATTN, D_MODEL), jnp.float32) * D_ATTN ** -0.5,
        "w_o": n(ks[21], (L, D_MODEL, D_MODEL), jnp.float32) * D_MODEL ** -0.5,
        "norm2_g": 1.0 + 0.05 * n(ks[22], (L, D_MODEL), jnp.float32),
        "w_route_group": n(ks[23], (L, D_MODEL, N_GROUPS), jnp.float32) * D_MODEL ** -0.5,
        "b_route_group": 0.01 * n(ks[24], (L, N_GROUPS), jnp.float32),
        "w_route_expert": n(ks[25], (L, D_MODEL, N_EXPERTS), jnp.float32) * D_MODEL ** -0.5,
        "b_route_expert": 0.01 * n(ks[26], (L, N_EXPERTS), jnp.float32),
        "w_gate_e": n(ks[27], (L, N_EXPERTS, D_MODEL, D_EXPERT), jnp.float32) * D_MODEL ** -0.5,
        "w_up_e": n(ks[28], (L, N_EXPERTS, D_MODEL, D_EXPERT), jnp.float32) * D_MODEL ** -0.5,
        "w_down_e": n(ks[29], (L, N_EXPERTS, D_EXPERT, D_MODEL), jnp.float32) * D_EXPERT ** -0.5,
    }


def reference(x_prompt, x_sample, cache_k, cache_v, cache_logf, state_conv, state_rglru,
              norm1_g, w_in, conv_w, conv_b, w_a, b_a, w_x, b_x, lru_lambda, b_f, q_norm_g, k_norm_g,
              w_rnn_out, w_attn_out, w_o, norm2_g, w_route_group, b_route_group, w_route_expert, b_route_expert,
              w_gate_e, w_up_e, w_down_e):
    Bp = x_prompt.shape[0]
    yp, ys = x_prompt, x_sample
    kp, vp, lfp, cvp, hp = [], [], [], [], []
    kss, vss, lfs, cvs, hs = [], [], [], [], []
    for l in range(DEPTH):
        lp = (norm1_g[l], w_in[l], conv_w[l], conv_b[l], w_a[l], b_a[l], w_x[l], b_x[l], lru_lambda[l], b_f[l],
              q_norm_g[l], k_norm_g[l], w_rnn_out[l], w_attn_out[l], w_o[l], norm2_g[l],
              w_route_group[l], b_route_group[l], w_route_expert[l], b_route_expert[l],
              w_gate_e[l], w_up_e[l], w_down_e[l])
        yp, c1, h1, k1, v1, f1 = _layer(
            yp, jnp.zeros((Bp, CONV_W - 1, D_RNN), yp.dtype), jnp.zeros((Bp, D_RNN), jnp.float32),
            jnp.zeros((Bp, 0, N_HEADS, HEAD_DIM), yp.dtype), jnp.zeros((Bp, 0, N_HEADS, HEAD_DIM), yp.dtype),
            jnp.zeros((Bp, 0, N_HEADS), jnp.float32), *lp)
        ys, c2, h2, k2, v2, f2 = _layer(
            ys, state_conv[l], state_rglru[l], cache_k[l], cache_v[l], cache_logf[l], *lp)
        kp.append(k1); vp.append(v1); lfp.append(f1); cvp.append(c1); hp.append(h1)
        kss.append(k2); vss.append(v2); lfs.append(f2); cvs.append(c2); hs.append(h2)
    return (yp, ys,
            jnp.stack(kp), jnp.stack(vp), jnp.stack(lfp), jnp.stack(cvp), jnp.stack(hp),
            jnp.stack(kss), jnp.stack(vss), jnp.stack(lfs), jnp.stack(cvs), jnp.stack(hs))
```

```python
import functools
import math

import jax
import jax.numpy as jnp
from jax import lax
from jax.experimental import pallas as pl
from jax.experimental.pallas import tpu as pltpu

F32 = jnp.float32
BF16 = jnp.bfloat16

D_MODEL = 1024
D_RNN = 1024
N_RNN_HEADS = 16
CONV_W = 4
LRU_C = 8.0
N_HEADS = 16
HEAD_DIM = 64
D_ATTN = N_HEADS * HEAD_DIM
N_GROUPS = 4
EXPERTS_PER_GROUP = 8
N_EXPERTS = N_GROUPS * EXPERTS_PER_GROUP
TOP_K = 2
D_EXPERT = 512
EPS = 1e-6
NEG_INF = -1e30
LOG2E = math.log2(math.e)

LANES = 128
SUBLANES = 8
MXU_DIM = 256
VMEM_LIMIT_BYTES = 56 * 1024 * 1024

N_PARTS = 7
ROUTE_LANES = 128
EXPERT_LANE0 = N_GROUPS
ROUTE_OUT = 8


def _params(semantics, **kw):
    return pltpu.CompilerParams(dimension_semantics=semantics, vmem_limit_bytes=VMEM_LIMIT_BYTES, **kw)


def _const_spec(shape):
    nd = len(shape)
    return pl.BlockSpec(shape, lambda *_: (0,) * nd, pipeline_mode=pl.Buffered(1))


def _log_sigmoid(x):
    return jnp.minimum(x, 0.0) - jnp.log1p(jnp.exp(-jnp.abs(x)))


def _softplus(x):
    return jnp.maximum(x, 0.0) + jnp.log1p(jnp.exp(-jnp.abs(x)))


def _neg_expm1(x):
    series = -x * (1.0 + x * (1 / 2) * (1.0 + x * (1 / 3) * (1.0 + x * (1 / 4) * (1.0 + x * (1 / 5) * (1.0 + x * (1 / 6))))))
    return jnp.where(x > -0.125, series, 1.0 - jnp.exp(x))


def _split_bf16(x):
    hi = x.astype(BF16)
    lo = (x - hi.astype(F32)).astype(BF16)
    return hi, lo


def _proj_kernel(x_ref, g1_ref, w_ref, wfh_ref, wfl_ref, bf_ref, qg_ref, kg_ref, seg_ref, segt_ref,
                 ux_ref, gg_ref, q_ref, k_ref, kb_ref, v_ref, vb_ref, sr_ref, sa_ref, lft_ref):
    x = x_ref[...]
    xn = x * lax.rsqrt(jnp.mean(x * x, axis=-1, keepdims=True) + EPS) * g1_ref[...]
    xb, xlo = _split_bf16(xn)

    def part(j):
        return jnp.dot(xb, w_ref[:, j * D_MODEL:(j + 1) * D_MODEL], preferred_element_type=F32)

    def head_norm(z):
        ssq = jnp.dot((z * z).astype(BF16), seg_ref[...], preferred_element_type=F32)
        inv = lax.rsqrt(ssq * (1.0 / HEAD_DIM) + EPS)
        inv_hi, inv_lo = _split_bf16(inv)
        inv_full = (jnp.dot(inv_hi, segt_ref[...], preferred_element_type=F32)
                    + jnp.dot(inv_lo, segt_ref[...], preferred_element_type=F32))
        return z * inv_full

    ux_ref[...] = part(0)
    gg_ref[...] = jax.nn.gelu(part(1)).astype(BF16)
    q_ref[...] = (head_norm(part(2)) * qg_ref[...]).astype(BF16)
    k = head_norm(part(3)) * kg_ref[...]
    k_ref[...] = k
    kb_ref[...] = k.astype(BF16)
    v = part(4)
    v_ref[...] = v
    vb_ref[...] = v.astype(BF16)
    sr_ref[...] = jax.nn.sigmoid(part(5)).astype(BF16)
    sa_ref[...] = jax.nn.sigmoid(part(6)).astype(BF16)

    nt = (((1,), (1,)), ((), ()))
    ft = (lax.dot_general(wfh_ref[...], xb, nt, preferred_element_type=F32)
          + lax.dot_general(wfl_ref[...], xb, nt, preferred_element_type=F32)
          + lax.dot_general(wfh_ref[...], xlo, nt, preferred_element_type=F32))
    lft_ref[...] = _log_sigmoid(ft + bf_ref[...])


def _proj(x2d, p, tm):
    n = x2d.shape[0]
    row = lambda dt: jax.ShapeDtypeStruct((n, D_MODEL), dt)
    rspec = pl.BlockSpec((tm, D_MODEL), lambda i: (i, 0))
    outs = [row(F32), row(BF16), row(BF16), row(F32), row(BF16), row(F32), row(BF16), row(BF16), row(BF16),
            jax.ShapeDtypeStruct((N_HEADS, n), F32)]
    return pl.pallas_call(
        _proj_kernel,
        grid=(n // tm,),
        in_specs=[rspec,
                  _const_spec((1, D_MODEL)),
                  _const_spec((D_MODEL, N_PARTS * D_MODEL)),
                  _const_spec((N_HEADS, D_MODEL)), _const_spec((N_HEADS, D_MODEL)),
                  _const_spec((N_HEADS, 1)),
                  _const_spec((1, D_MODEL)), _const_spec((1, D_MODEL)),
                  _const_spec((D_MODEL, N_HEADS)), _const_spec((N_HEADS, D_MODEL))],
        out_specs=[rspec] * 9 + [pl.BlockSpec((N_HEADS, tm), lambda i: (0, i))],
        out_shape=outs,
        compiler_params=_params(("arbitrary",)),
        name="proj",
    )(x2d, p["g1"], p["w_main"], p["wft_hi"], p["wft_lo"], p["bf_col"], p["qg"], p["kg"], p["seg"], p["segt"])


def _cum_kernel(lf_ref, c_ref):
    length = lf_ref.shape[-1]
    ch = MXU_DIM
    r = lax.broadcasted_iota(jnp.int32, (ch, ch), 0)
    c = lax.broadcasted_iota(jnp.int32, (ch, ch), 1)
    upper = (r <= c).astype(BF16)
    carry = jnp.zeros((N_HEADS, 1), F32)
    for i in range(length // ch):
        xx = lf_ref[0, :, i * ch:(i + 1) * ch]
        hi = xx.astype(BF16)
        r1 = xx - hi.astype(F32)
        mid = r1.astype(BF16)
        lo = (r1 - mid.astype(F32)).astype(BF16)
        s = (jnp.dot(hi, upper, preferred_element_type=F32)
             + jnp.dot(mid, upper, preferred_element_type=F32)
             + jnp.dot(lo, upper, preferred_element_type=F32)) + carry
        c_ref[0, :, i * ch:(i + 1) * ch] = s * LOG2E
        carry = s[:, ch - 1:ch]


def _cum(lft):
    b, h, length = lft.shape
    spec = pl.BlockSpec((1, h, length), lambda i: (i, 0, 0))
    return pl.pallas_call(
        _cum_kernel, grid=(b,), in_specs=[spec], out_specs=spec,
        out_shape=jax.ShapeDtypeStruct(lft.shape, F32),
        compiler_params=_params(("arbitrary",)), name="cum",
    )(lft)


def _rnn_kernel(ux_ref, gg_ref, sr_ref, hist_ref, h0_ref, cw_ref, cb_ref, wa_ref, wx_ref, ba_ref, bx_ref,
                lam_ref, wo_ref, rp_ref, tail_ref, hl_ref, tail_sc, h_sc):
    t = pl.program_id(1)
    tt = ux_ref.shape[1]

    @pl.when(t == 0)
    def _():
        tail_sc[...] = hist_ref[0]
        h_sc[...] = h0_ref[0]

    u = ux_ref[0]
    prev = tail_sc[...]
    row8 = lax.broadcasted_iota(jnp.int32, (SUBLANES, 1), 0)

    def shifted(k):
        ru = pltpu.roll(u, k, axis=0)
        rp = pltpu.roll(prev, k, axis=0)
        first = jnp.where(row8 < k, rp, ru[0:SUBLANES])
        return jnp.concatenate([first, ru[SUBLANES:]], axis=0)

    cw = cw_ref[...]
    conv = (cw[3:4] * u + cw[2:3] * shifted(1) + cw[1:2] * shifted(2) + cw[0:1] * shifted(3)) + cb_ref[...]
    tail_sc[...] = u[tt - SUBLANES:tt]

    cb16 = conv.astype(BF16)
    n_blk = D_RNN // MXU_DIM

    def gate(w_ref, b_ref):
        cols = [jnp.dot(cb16[:, j * MXU_DIM:(j + 1) * MXU_DIM], w_ref[j], preferred_element_type=F32)
                for j in range(n_blk)]
        return jax.nn.sigmoid(jnp.concatenate(cols, axis=1) + b_ref[...])

    r = gate(wa_ref, ba_ref)
    i = gate(wx_ref, bx_ref)
    log_a = (-LRU_C) * r * _softplus(-lam_ref[...])
    a = jnp.exp(log_a)
    b = jnp.sqrt(_neg_expm1(2.0 * log_a)) * (i * conv)

    row = lax.broadcasted_iota(jnp.int32, (tt, 1), 0)
    s = 1
    while s < tt:
        m = row >= s
        a_sh = jnp.where(m, pltpu.roll(a, s, axis=0), 1.0)
        b_sh = jnp.where(m, pltpu.roll(b, s, axis=0), 0.0)
        b = b + a * b_sh
        a = a * a_sh
        s *= 2
    h_prev = h_sc[SUBLANES - 1:SUBLANES, :]
    h = a * h_prev + b
    h_sc[...] = h[tt - SUBLANES:tt]

    y_in = (gg_ref[0].astype(F32) * h).astype(BF16)
    y = jnp.dot(y_in, wo_ref[...], preferred_element_type=F32)
    rp_ref[0] = (sr_ref[0].astype(F32) * y).astype(BF16)
    tail_ref[0] = tail_sc[...]
    hl_ref[0] = h_sc[...]


def _rnn(ux, gg, sr, hist8, h08, p, tt):
    b, t, _ = ux.shape
    tspec = pl.BlockSpec((1, tt, D_RNN), lambda i, j: (i, j, 0))
    sspec = pl.BlockSpec((1, SUBLANES, D_RNN), lambda i, j: (i, 0, 0))
    nb = D_RNN // MXU_DIM
    return pl.pallas_call(
        _rnn_kernel,
        grid=(b, t // tt),
        in_specs=[tspec, tspec, tspec, sspec, sspec,
                  _const_spec((CONV_W, D_RNN)), _const_spec((1, D_RNN)),
                  _const_spec((nb, MXU_DIM, MXU_DIM)), _const_spec((nb, MXU_DIM, MXU_DIM)),
                  _const_spec((1, D_RNN)), _const_spec((1, D_RNN)), _const_spec((1, D_RNN)),
                  _const_spec((D_RNN, D_MODEL))],
        out_specs=[tspec, sspec, sspec],
        out_shape=[jax.ShapeDtypeStruct((b, t, D_MODEL), BF16),
                   jax.ShapeDtypeStruct((b, SUBLANES, D_RNN), F32),
                   jax.ShapeDtypeStruct((b, SUBLANES, D_RNN), F32)],
        scratch_shapes=[pltpu.VMEM((SUBLANES, D_RNN), F32), pltpu.VMEM((SUBLANES, D_RNN), F32)],
        compiler_params=_params(("arbitrary", "arbitrary")),
        name="rnn",
    )(ux, gg, sr, hist8, h08, p["conv_w"], p["conv_b"], p["wa_bd"], p["wx_bd"], p["b_a"], p["b_x"],
      p["lam"], p["w_rnn_out"])


def _attn_kernel(q_ref, k_ref, v_ref, cq_ref, ck_ref, o_ref, *, past, tk):
    qi = pl.program_id(2)
    hp = pl.program_id(1)
    tq = q_ref.shape[1]
    q_pos0 = past + qi * tq
    n_full = (q_pos0 + 1) // tk
    n_tot = (q_pos0 + tq - 1) // tk + 1
    lane_h = lax.broadcasted_iota(jnp.int32, (tq, N_HEADS), 1)
    nt = (((1,), (1,)), ((), ()))

    for hh in range(LANES // HEAD_DIM):
        head = hp * (LANES // HEAD_DIM) + hh
        lo = hh * HEAD_DIM
        q = q_ref[0, :, lo:lo + HEAD_DIM]
        cq = jnp.sum(jnp.where(lane_h == head, cq_ref[0], 0.0), axis=-1, keepdims=True)

        def step(j, carry, masked):
            m, l, acc = carry
            start = pl.multiple_of(j * tk, tk)
            k = k_ref[0, pl.ds(start, tk), lo:lo + HEAD_DIM]
            v = v_ref[0, pl.ds(start, tk), lo:lo + HEAD_DIM]
            ck = ck_ref[0, pl.ds(head, 1), pl.ds(start, tk)]
            s = lax.dot_general(q, k, nt, preferred_element_type=F32) + (cq - ck)
            if masked:
                kpos = start + lax.broadcasted_iota(jnp.int32, (tq, tk), 1)
                qpos = q_pos0 + lax.broadcasted_iota(jnp.int32, (tq, tk), 0)
                s = jnp.where(kpos <= qpos, s, NEG_INF)
            m_new = jnp.maximum(m, jnp.max(s, axis=-1, keepdims=True))
            alpha = jnp.exp2(m - m_new)
            pr = jnp.exp2(s - m_new)
            l = alpha * l + jnp.sum(pr, axis=-1, keepdims=True)
            acc = alpha * acc + jnp.dot(pr.astype(BF16), v, preferred_element_type=F32)
            return m_new, l, acc

        init = (jnp.full((tq, 1), NEG_INF, F32), jnp.zeros((tq, 1), F32), jnp.zeros((tq, HEAD_DIM), F32))
        carry = lax.fori_loop(0, n_full, functools.partial(step, masked=False), init)
        m, l, acc = lax.fori_loop(n_full, n_tot, functools.partial(step, masked=True), carry)
        o_ref[0, :, lo:lo + HEAD_DIM] = (acc / l).astype(o_ref.dtype)


def _attn(qb, kb, vb, cq, ckt, past, tq, tk):
    b, t, _ = qb.shape
    length = kb.shape[1]
    hpairs = D_ATTN // LANES
    qspec = pl.BlockSpec((1, tq, LANES), lambda i, h, j: (i, j, h))
    kspec = pl.BlockSpec((1, length, LANES), lambda i, h, j: (i, 0, h))
    return pl.pallas_call(
        functools.partial(_attn_kernel, past=past, tk=tk),
        grid=(b, hpairs, t // tq),
        in_specs=[qspec, kspec, kspec,
                  pl.BlockSpec((1, tq, N_HEADS), lambda i, h, j: (i, j, 0)),
                  pl.BlockSpec((1, N_HEADS, length), lambda i, h, j: (i, 0, 0))],
        out_specs=qspec,
        out_shape=jax.ShapeDtypeStruct((b, t, D_ATTN), BF16),
        compiler_params=_params(("arbitrary", "arbitrary", "arbitrary")),
        name="attn",
    )(qb, kb, vb, cq, ckt)


def _post_kernel(x_ref, at_ref, rp_ref, sa_ref, wao_ref, wo_ref, g2_ref, wrh_ref, wrl_ref, br_ref,
                 x1_ref, xn_ref, route_ref, cnt_ref, cnt_sc):
    step = pl.program_id(0)
    tm = x_ref.shape[0]

    @pl.when(step == 0)
    def _():
        cnt_sc[...] = jnp.zeros_like(cnt_sc)

    attn_y = jnp.dot(at_ref[...], wao_ref[...], preferred_element_type=F32)
    merged = rp_ref[...].astype(F32) + sa_ref[...].astype(F32) * attn_y
    x1 = x_ref[...] + jnp.dot(merged.astype(BF16), wo_ref[...], preferred_element_type=F32)
    x1_ref[...] = x1
    xn = x1 * lax.rsqrt(jnp.mean(x1 * x1, axis=-1, keepdims=True) + EPS) * g2_ref[...]
    xh, xl = _split_bf16(xn)
    xn_ref[...] = xn

    lg = (jnp.dot(xh, wrh_ref[...], preferred_element_type=F32)
          + jnp.dot(xh, wrl_ref[...], preferred_element_type=F32)
          + jnp.dot(xl, wrh_ref[...], preferred_element_type=F32)) + br_ref[...]
    lane = lax.broadcasted_iota(jnp.int32, (tm, ROUTE_LANES), 1)
    lane_f = lane.astype(F32)
    big = float(ROUTE_LANES)

    def first_argmax(vals):
        top = jnp.max(vals, axis=-1, keepdims=True)
        idx = jnp.min(jnp.where(vals == top, lane_f, big), axis=-1, keepdims=True)
        return top, idx

    is_grp = lane < N_GROUPS
    gl = jnp.where(is_grp, lg, NEG_INF)
    gmax, grp = first_argmax(gl)
    g_w = 1.0 / jnp.sum(jnp.where(is_grp, jnp.exp(gl - gmax), 0.0), axis=-1, keepdims=True)
    e_lo = EXPERT_LANE0 + EXPERTS_PER_GROUP * grp
    in_grp = (lane_f >= e_lo) & (lane_f < e_lo + EXPERTS_PER_GROUP)
    el = jnp.where(in_grp, lg, NEG_INF)
    t1, i1 = first_argmax(el)
    t2, i2 = first_argmax(jnp.where(lane_f == i1, NEG_INF, el))
    e21 = jnp.exp(t2 - t1)
    den = 1.0 + e21
    gate1 = g_w / den
    gate2 = g_w * e21 / den

    onehot = ((lane_f == i1) | (lane_f == i2)).astype(F32)
    rr = lax.broadcasted_iota(jnp.int32, (tm, tm), 0)
    cc = lax.broadcasted_iota(jnp.int32, (tm, tm), 1)
    lower = (rr > cc).astype(BF16)
    before = jnp.dot(lower, onehot.astype(BF16), preferred_element_type=F32) + cnt_sc[...]
    rank1 = jnp.sum(jnp.where(lane_f == i1, before, 0.0), axis=-1, keepdims=True)
    rank2 = jnp.sum(jnp.where(lane_f == i2, before, 0.0), axis=-1, keepdims=True)
    cnt_sc[...] = cnt_sc[...] + jnp.sum(onehot, axis=0, keepdims=True)
    cnt_ref[...] = cnt_sc[...]

    out_lane = lax.broadcasted_iota(jnp.int32, (tm, ROUTE_OUT), 1)
    route = jnp.zeros((tm, ROUTE_OUT), F32)
    for j, col in enumerate((i1 - EXPERT_LANE0, i2 - EXPERT_LANE0, gate1, gate2, rank1, rank2)):
        route = jnp.where(out_lane == j, col, route)
    route_ref[...] = route


def _post(x2d, attn2d, rp2d, sa2d, p, tm):
    n = x2d.shape[0]
    rspec = pl.BlockSpec((tm, D_MODEL), lambda i: (i, 0))
    return pl.pallas_call(
        _post_kernel,
        grid=(n // tm,),
        in_specs=[rspec, rspec, rspec, rspec,
                  _const_spec((D_ATTN, D_MODEL)), _const_spec((D_MODEL, D_MODEL)), _const_spec((1, D_MODEL)),
                  _const_spec((D_MODEL, ROUTE_LANES)), _const_spec((D_MODEL, ROUTE_LANES)),
                  _const_spec((1, ROUTE_LANES))],
        out_specs=[rspec, rspec, pl.BlockSpec((tm, ROUTE_OUT), lambda i: (i, 0)),
                   pl.BlockSpec((1, ROUTE_LANES), lambda i: (0, 0))],
        out_shape=[jax.ShapeDtypeStruct((n, D_MODEL), F32), jax.ShapeDtypeStruct((n, D_MODEL), F32),
                   jax.ShapeDtypeStruct((n, ROUTE_OUT), F32), jax.ShapeDtypeStruct((1, ROUTE_LANES), F32)],
        scratch_shapes=[pltpu.VMEM((1, ROUTE_LANES), F32)],
        compiler_params=_params(("arbitrary",)),
        name="post",
    )(x2d, attn2d, rp2d, sa2d, p["w_attn_out"], p["w_o"], p["g2"], p["wr_hi"], p["wr_lo"], p["br"])


def _row_copy(src_ref, src_row, dst_ref, dst_row, sem):
    return pltpu.make_async_copy(src_ref.at[pl.ds(src_row, 1), :], dst_ref.at[pl.ds(dst_row, 1), :], sem)


def _dispatch_kernel(dest_ref, xn_ref, zeros_ref, xs_ref, sem):
    del zeros_ref
    tm = xn_ref.shape[0]

    def start(r, c):
        for k in range(TOP_K):
            _row_copy(xn_ref, r, xs_ref, dest_ref[0, 0, TOP_K * r + k], sem).start()
        return c

    lax.fori_loop(0, tm, start, 0)

    def wait(r, c):
        for k in range(TOP_K):
            _row_copy(xn_ref, r, xs_ref, dest_ref[0, 0, TOP_K * r + k], sem).wait()
        return c

    lax.fori_loop(0, tm, wait, 0)


def _dispatch(xn2d, dest, n_rows, tm):
    n = xn2d.shape[0]
    dest3 = dest.reshape(n // tm, 1, TOP_K * tm)
    zeros = jnp.zeros((n_rows, D_MODEL), F32)
    return pl.pallas_call(
        _dispatch_kernel,
        grid=(n // tm,),
        in_specs=[pl.BlockSpec((1, 1, TOP_K * tm), lambda i: (i, 0, 0), memory_space=pltpu.SMEM),
                  pl.BlockSpec((tm, D_MODEL), lambda i: (i, 0)),
                  pl.BlockSpec(memory_space=pl.ANY)],
        out_specs=pl.BlockSpec(memory_space=pl.ANY),
        out_shape=jax.ShapeDtypeStruct((n_rows, D_MODEL), F32),
        scratch_shapes=[pltpu.SemaphoreType.DMA(())],
        input_output_aliases={2: 0},
        compiler_params=_params(("arbitrary",), has_side_effects=True),
        name="dispatch",
    )(dest3, xn2d, zeros)


def _gmm_kernel(blk_e_ref, n_used_ref, xs_ref, wg_ref, wu_ref, wd_ref, y_ref):
    del blk_e_ref
    i = pl.program_id(0)

    @pl.when(i < n_used_ref[0])
    def _():
        x = xs_ref[...].astype(BF16)
        g = jnp.dot(x, wg_ref[0], preferred_element_type=F32)
        u = jnp.dot(x, wu_ref[0], preferred_element_type=F32)
        h = (g * jax.nn.sigmoid(g) * u).astype(BF16)
        y_ref[...] = jnp.dot(h, wd_ref[0], preferred_element_type=F32)

    @pl.when(i >= n_used_ref[0])
    def _():
        y_ref[...] = jnp.zeros_like(y_ref)


def _gmm(xs, blk_e, n_used, p, blk):
    n_rows = xs.shape[0]
    nblk = n_rows // blk
    grid_spec = pltpu.PrefetchScalarGridSpec(
        num_scalar_prefetch=2,
        grid=(nblk,),
        in_specs=[pl.BlockSpec((blk, D_MODEL), lambda i, be, nu: (i, 0)),
                  pl.BlockSpec((1, D_MODEL, D_EXPERT), lambda i, be, nu: (be[i], 0, 0)),
                  pl.BlockSpec((1, D_MODEL, D_EXPERT), lambda i, be, nu: (be[i], 0, 0)),
                  pl.BlockSpec((1, D_EXPERT, D_MODEL), lambda i, be, nu: (be[i], 0, 0))],
        out_specs=pl.BlockSpec((blk, D_MODEL), lambda i, be, nu: (i, 0)),
    )
    return pl.pallas_call(
        _gmm_kernel, grid_spec=grid_spec,
        out_shape=jax.ShapeDtypeStruct((n_rows, D_MODEL), F32),
        compiler_params=_params(("arbitrary",)),
        name="gmm",
    )(blk_e, n_used, xs, p["w_gate_e"], p["w_up_e"], p["w_down_e"])


def _combine_kernel(dest_ref, x1_ref, route_ref, yb_ref, y_ref, buf, sem):
    tm = x1_ref.shape[0]

    def start(r, c):
        for k in range(TOP_K):
            _row_copy(yb_ref, dest_ref[0, 0, TOP_K * r + k], buf.at[k], r, sem).start()
        return c

    lax.fori_loop(0, tm, start, 0)

    def wait(r, c):
        for k in range(TOP_K):
            _row_copy(yb_ref, dest_ref[0, 0, TOP_K * r + k], buf.at[k], r, sem).wait()
        return c

    lax.fori_loop(0, tm, wait, 0)
    route = route_ref[...]
    y_ref[...] = x1_ref[...] + route[:, 2:3] * buf[0] + route[:, 3:4] * buf[1]


def _combine(x1, route, yb, dest, tm):
    n = x1.shape[0]
    dest3 = dest.reshape(n // tm, 1, TOP_K * tm)
    return pl.pallas_call(
        _combine_kernel,
        grid=(n // tm,),
        in_specs=[pl.BlockSpec((1, 1, TOP_K * tm), lambda i: (i, 0, 0), memory_space=pltpu.SMEM),
                  pl.BlockSpec((tm, D_MODEL), lambda i: (i, 0)),
                  pl.BlockSpec((tm, ROUTE_OUT), lambda i: (i, 0)),
                  pl.BlockSpec(memory_space=pl.ANY)],
        out_specs=pl.BlockSpec((tm, D_MODEL), lambda i: (i, 0)),
        out_shape=jax.ShapeDtypeStruct((n, D_MODEL), F32),
        scratch_shapes=[pltpu.VMEM((TOP_K, tm, D_MODEL), F32), pltpu.SemaphoreType.DMA(())],
        compiler_params=_params(("arbitrary",)),
        name="combine",
    )(dest3, x1, route, yb)


def _block_diag(w):
    per = MXU_DIM // (D_RNN // N_RNN_HEADS)
    blk = D_RNN // N_RNN_HEADS
    w4 = w.reshape(D_RNN // MXU_DIM, per, blk, blk)
    eye = jnp.eye(per, dtype=w.dtype)
    return jnp.einsum("cpij,pq->cpiqj", w4, eye).reshape(D_RNN // MXU_DIM, MXU_DIM, MXU_DIM)


def _prep_params(norm1_g, w_in, conv_w, conv_b, w_a, b_a, w_x, b_x, lru_lambda, b_f, q_norm_g, k_norm_g,
                 w_rnn_out, w_attn_out, w_o, norm2_g, w_route_group, b_route_group, w_route_expert,
                 b_route_expert, w_gate_e, w_up_e, w_down_e):
    f0 = 2 * D_RNN + 3 * D_ATTN
    w_main = jnp.concatenate([w_in[:, :f0], w_in[:, f0 + N_HEADS:]], axis=1).astype(BF16)
    wft = w_in[:, f0:f0 + N_HEADS].T
    wft_hi, wft_lo = _split_bf16(wft)
    head_of = jnp.arange(D_ATTN) // HEAD_DIM
    seg = (head_of[:, None] == jnp.arange(N_HEADS)[None, :]).astype(BF16)
    w_route = jnp.concatenate([w_route_group, w_route_expert,
                               jnp.zeros((D_MODEL, ROUTE_LANES - N_GROUPS - N_EXPERTS), F32)], axis=1)
    wr_hi, wr_lo = _split_bf16(w_route)
    br = jnp.concatenate([b_route_group, b_route_expert,
                          jnp.zeros((ROUTE_LANES - N_GROUPS - N_EXPERTS,), F32)])[None, :]
    return dict(
        g1=norm1_g[None, :], w_main=w_main, wft_hi=wft_hi, wft_lo=wft_lo, bf_col=b_f[:, None],
        qg=jnp.tile(q_norm_g, N_HEADS)[None, :] * (HEAD_DIM ** -0.5 * LOG2E),
        kg=jnp.tile(k_norm_g, N_HEADS)[None, :],
        seg=seg, segt=seg.T,
        conv_w=conv_w, conv_b=conv_b[None, :], wa_bd=_block_diag(w_a).astype(BF16),
        wx_bd=_block_diag(w_x).astype(BF16), b_a=b_a[None, :], b_x=b_x[None, :], lam=lru_lambda[None, :],
        w_rnn_out=w_rnn_out.astype(BF16), w_attn_out=w_attn_out.astype(BF16), w_o=w_o.astype(BF16),
        g2=norm2_g[None, :], wr_hi=wr_hi, wr_lo=wr_lo, br=br,
        w_gate_e=w_gate_e.astype(BF16), w_up_e=w_up_e.astype(BF16), w_down_e=w_down_e.astype(BF16),
    )


def _pick(n, prefs):
    for c in prefs:
        if n % c == 0:
            return c
    raise ValueError(f"no tile for {n}")


def _moe(x1, xn2, route, counts_row, p):
    n = x1.shape[0]
    blk = _pick(TOP_K * n, (512, 256, 128)) if n >= 8192 else 128
    tm = _pick(n, (256, 128, 64, 8))
    eid = route[:, 0:TOP_K].astype(jnp.int32)
    rank = route[:, 4:4 + TOP_K].astype(jnp.int32)
    counts = counts_row[0, EXPERT_LANE0:EXPERT_LANE0 + N_EXPERTS].astype(jnp.int32)
    padded = (counts + blk - 1) // blk * blk
    pend = jnp.cumsum(padded)
    pstart = pend - padded
    dest = (pstart[eid] + rank).astype(jnp.int32)
    nblk = (TOP_K * n) // blk + N_EXPERTS
    blk_e = jnp.minimum(jnp.searchsorted(pend, jnp.arange(nblk, dtype=jnp.int32) * blk, side="right"),
                        N_EXPERTS - 1).astype(jnp.int32)
    n_used = (pend[-1:] // blk).astype(jnp.int32)
    xs = _dispatch(xn2, dest, nblk * blk, tm)
    yb = _gmm(xs, blk_e, n_used, p, blk)
    return _combine(x1, route, yb, dest, tm)


def _layer(x, conv_hist, h0, k_hist, v_hist, lf_hist, p):
    b, t, _ = x.shape
    past = k_hist.shape[1]
    n = b * t
    x2d = x.reshape(n, D_MODEL)
    tm = _pick(n, (256, 128, 64, 8))
    ux, gg, qb, k, kb, v, vb, sr, sa, lft = _proj(x2d, p, tm)
    r3 = lambda a: a.reshape(b, t, D_MODEL)

    tt = _pick(t, (256, 128, 64, 8))
    hist8 = jnp.concatenate([jnp.zeros((b, SUBLANES - (CONV_W - 1), D_RNN), F32), conv_hist.astype(F32)], axis=1)
    h08 = jnp.broadcast_to(h0.astype(F32)[:, None, :], (b, SUBLANES, D_RNN))
    rp, tail8, hl8 = _rnn(r3(ux), r3(gg), r3(sr), hist8, h08, p, tt)
    new_conv = tail8[:, SUBLANES - (CONV_W - 1):]
    h_last = hl8[:, SUBLANES - 1]

    tk = MXU_DIM
    tq = _pick(t, (256, 128, 64, 8))
    lf_new_t = lft.reshape(N_HEADS, b, t).transpose(1, 0, 2)
    length = past + t
    lpad = (length + tk - 1) // tk * tk
    lf_all_t = jnp.concatenate([lf_hist.astype(F32).transpose(0, 2, 1), lf_new_t,
                                jnp.zeros((b, N_HEADS, lpad - length), F32)], axis=2)
    ckt = _cum(lf_all_t)
    cq = ckt[:, :, past:past + t].transpose(0, 2, 1)
    if past:
        kb_all = jnp.concatenate([k_hist.reshape(b, past, D_ATTN).astype(BF16), r3(kb)], axis=1)
        vb_all = jnp.concatenate([v_hist.reshape(b, past, D_ATTN).astype(BF16), r3(vb)], axis=1)
    else:
        kb_all, vb_all = r3(kb), r3(vb)
    if lpad != length:
        pad = jnp.zeros((b, lpad - length, D_ATTN), BF16)
        kb_all = jnp.concatenate([kb_all, pad], axis=1)
        vb_all = jnp.concatenate([vb_all, pad], axis=1)
    attn = _attn(r3(qb), kb_all, vb_all, cq, ckt, past, tq, tk)

    x1, xn2, route, counts = _post(x2d, attn.reshape(n, D_ATTN), rp.reshape(n, D_MODEL), sa, p, tm)
    y = _moe(x1, xn2, route, counts, p)
    lf_new = lf_new_t.transpose(0, 2, 1)
    return (y.reshape(b, t, D_MODEL), new_conv, h_last,
            k.reshape(b, t, N_HEADS, HEAD_DIM), v.reshape(b, t, N_HEADS, HEAD_DIM), lf_new)


def kernel(x_prompt, x_sample, cache_k, cache_v, cache_logf, state_conv, state_rglru, norm1_g, w_in, conv_w,
           conv_b, w_a, b_a, w_x, b_x, lru_lambda, b_f, q_norm_g, k_norm_g, w_rnn_out, w_attn_out, w_o,
           norm2_g, w_route_group, b_route_group, w_route_expert, b_route_expert, w_gate_e, w_up_e, w_down_e):
    depth = norm1_g.shape[0]
    bp = x_prompt.shape[0]
    yp, ys = x_prompt, x_sample
    outs_p, outs_s = [], []
    for l in range(depth):
        p = _prep_params(norm1_g[l], w_in[l], conv_w[l], conv_b[l], w_a[l], b_a[l], w_x[l], b_x[l],
                         lru_lambda[l], b_f[l], q_norm_g[l], k_norm_g[l], w_rnn_out[l], w_attn_out[l], w_o[l],
                         norm2_g[l], w_route_group[l], b_route_group[l], w_route_expert[l], b_route_expert[l],
                         w_gate_e[l], w_up_e[l], w_down_e[l])
        yp, *rest_p = _layer(yp, jnp.zeros((bp, CONV_W - 1, D_RNN), F32), jnp.zeros((bp, D_RNN), F32),
                             jnp.zeros((bp, 0, N_HEADS, HEAD_DIM), F32), jnp.zeros((bp, 0, N_HEADS, HEAD_DIM), F32),
                             jnp.zeros((bp, 0, N_HEADS), F32), p)
        ys, *rest_s = _layer(ys, state_conv[l], state_rglru[l], cache_k[l], cache_v[l], cache_logf[l], p)
        outs_p.append(rest_p)
        outs_s.append(rest_s)
    stack = lambda outs, j: jnp.stack([o[j] for o in outs])
    order = (2, 3, 4, 0, 1)
    return (yp, ys, *[stack(outs_p, j) for j in order], *[stack(outs_s, j) for j in order])
```
